```python
import math
import jax, jax.numpy as jnp
from jax import lax
import numpy as np

D_MODEL = 1024
BATCH = 1
SEQ = 16384
DEPTH = 1
DEC_BATCH = 32
DEC_SEQ = 8
PAST_LEN = 16384
PAGE_SIZE = 128

H_A = 8
HD_A = 64
H_IDX = 4
D_IDX = 64
TOPK_MAX = 256
H_B = 4
HD_B = 64
DV_B = 2 * HD_B
D_FF = 2816
CONV_W = 3
ROPE_THETA = 10000.0
EPS = 1e-6
Q_BLOCK = 128
W_A = H_A * HD_A
W_B = H_B * DV_B
SPLITS = [W_A, W_A, W_A, H_IDX * D_IDX, D_IDX, H_IDX,
          H_B * 2 * HD_B, H_B * 2 * HD_B, W_B, D_MODEL, D_MODEL]
N_IN = sum(SPLITS)

kernel_name = "hybrid_dsa_diffattn_convffn_step"


def rms_norm(x, g):
    xf = x.astype(jnp.float32)
    y = xf * lax.rsqrt(jnp.mean(xf * xf, axis=-1, keepdims=True) + EPS)
    return (y * g.astype(jnp.float32)).astype(x.dtype)


def rope(x, pos):
    d = x.shape[-1]
    inv = ROPE_THETA ** (-jnp.arange(0, d, 2, dtype=jnp.float32) / d)
    ang = pos.astype(jnp.float32)[:, None] * inv
    ang = ang.reshape(ang.shape[:1] + (1,) * (x.ndim - 3) + ang.shape[1:])
    cos, sin = jnp.cos(ang), jnp.sin(ang)
    xf = x.astype(jnp.float32)
    x1, x2 = xf[..., : d // 2], xf[..., d // 2:]
    return jnp.concatenate([x1 * cos - x2 * sin, x2 * cos + x1 * sin], axis=-1).astype(x.dtype)


def project(h, w_in, pos):
    B, T, _ = h.shape
    z = h @ w_in
    cuts = [int(c) for c in np.cumsum(SPLITS)[:-1]]
    qa, ka, va, qi, ki, wi, qb, kb, vb, ga, gb = jnp.split(z, cuts, axis=-1)
    qa = rope(qa.reshape(B, T, H_A, HD_A), pos)
    ka = rope(ka.reshape(B, T, H_A, HD_A), pos)
    va = va.reshape(B, T, H_A, HD_A)
    qi = rope(qi.reshape(B, T, H_IDX, D_IDX), pos)
    ki = rope(ki, pos)
    wi = wi * (H_IDX ** -0.5 * D_IDX ** -0.5)
    qb = rope(qb.reshape(B, T, H_B, 2, HD_B), pos)
    kb = rope(kb.reshape(B, T, H_B, 2, HD_B), pos)
    vb = vb.reshape(B, T, H_B, DV_B)
    return qa, ka, va, qi, ki, wi, qb, kb, vb, ga, gb


batch_take = jax.vmap(lambda t, i: t[i])


def dsa_select(qi, wi, k_idx, q_pos, topk):
    L = k_idx.shape[1]
    s = jnp.einsum('bqhd,bld->bqhl', qi, k_idx).astype(jnp.float32)
    score = jnp.einsum('bqhl,bqh->bql', jax.nn.relu(s), wi.astype(jnp.float32))
    causal = jnp.arange(L)[None, :] <= q_pos[:, None]
    score = jnp.where(causal[None], score, -jnp.inf)
    _, idx = lax.top_k(score, topk)
    valid = idx <= q_pos[None, :, None]
    return idx, valid


def sparse_attend(qa, k_sel, v_sel, valid):
    s = jnp.einsum('bqhd,bqkhd->bqhk', qa, k_sel).astype(jnp.float32) * HD_A ** -0.5
    s = jnp.where(valid[:, :, None, :], s, -jnp.inf)
    p = jax.nn.softmax(s, axis=-1)
    return jnp.einsum('bqhk,bqkhd->bqhd', p.astype(v_sel.dtype), v_sel)


def diff_attend(qb, kb, vb, q_pos, lam, lam_init, g_subln):
    L = kb.shape[1]
    s = jnp.einsum('bqhmd,blhmd->bhmql', qb, kb).astype(jnp.float32) * HD_B ** -0.5
    mask = jnp.arange(L)[None, :] <= q_pos[:, None]
    s = jnp.where(mask, s, -jnp.inf)
    p = jax.nn.softmax(s, axis=-1)
    a = p[:, :, 0] - lam * p[:, :, 1]
    o = jnp.einsum('bhql,blhv->bqhv', a.astype(vb.dtype), vb)
    return rms_norm(o, g_subln) * (1.0 - lam_init)


def prompt_mix(qa, ka, va, qi, ki, wi, qb, kb, vb, lam, lam_init, g_subln):
    B, S = qa.shape[:2]
    topk = min(TOPK_MAX, S // 4)

    def one_block(i):
        start = i * Q_BLOCK
        sl = lambda t: lax.dynamic_slice_in_dim(t, start, Q_BLOCK, axis=1)
        q_pos = start + jnp.arange(Q_BLOCK, dtype=jnp.int32)
        idx, valid = dsa_select(sl(qi), sl(wi), ki, q_pos, topk)
        o_a = sparse_attend(sl(qa), batch_take(ka, idx), batch_take(va, idx), valid)
        o_b = diff_attend(sl(qb), kb, vb, q_pos, lam, lam_init, g_subln)
        return o_a, o_b

    o_a, o_b = lax.map(one_block, jnp.arange(S // Q_BLOCK, dtype=jnp.int32))
    o_a = jnp.moveaxis(o_a, 0, 1).reshape(B, S, W_A)
    o_b = jnp.moveaxis(o_b, 0, 1).reshape(B, S, W_B)
    return o_a, o_b


def sample_mix(qa, ka, va, qi, ki, wi, qb, kb, vb, lam, lam_init, g_subln,
               cache_k_a, cache_v_a, cache_k_idx, cache_k_b, cache_v_b, page_table, layer):
    DB, T = qa.shape[:2]
    past = page_table.shape[1] * PAGE_SIZE
    L = past + T
    topk = min(TOPK_MAX, L // 4)
    q_pos = past + jnp.arange(T, dtype=jnp.int32)
    ki_past = cache_k_idx[layer, page_table].reshape(DB, past, D_IDX)
    idx, valid = dsa_select(qi, wi, jnp.concatenate([ki_past, ki], axis=1), q_pos, topk)
    from_past = (idx < past)[..., None, None]
    pidx = jnp.minimum(idx, past - 1)
    phys = jax.vmap(lambda pt, i: pt[i // PAGE_SIZE])(page_table, pidx)
    slot = pidx % PAGE_SIZE
    nidx = jnp.clip(idx - past, 0, T - 1)
    k_sel = jnp.where(from_past, cache_k_a[layer, phys, slot], batch_take(ka, nidx))
    v_sel = jnp.where(from_past, cache_v_a[layer, phys, slot], batch_take(va, nidx))
    o_a = sparse_attend(qa, k_sel, v_sel, valid).reshape(DB, T, W_A)

    def one_seq(args):
        qb_s, kb_s, vb_s, pt = args
        kb_all = jnp.concatenate([cache_k_b[layer, pt].reshape(past, H_B, 2, HD_B), kb_s], axis=0)
        vb_all = jnp.concatenate([cache_v_b[layer, pt].reshape(past, H_B, DV_B), vb_s], axis=0)
        return diff_attend(qb_s[None], kb_all[None], vb_all[None], q_pos, lam, lam_init, g_subln)[0]

    o_b = lax.map(one_seq, (qb, kb, vb, page_table)).reshape(DB, T, W_B)
    return o_a, o_b


def conv_ffn(h, conv_state, w_g, w_u, conv_w, conv_b, w_d):
    T = h.shape[1]
    g = h @ w_g
    u = h @ w_u
    gx = jnp.concatenate([conv_state.astype(g.dtype), g], axis=1)
    c = conv_b
    for j in range(CONV_W):
        c = c + conv_w[j] * gx[:, j:j + T]
    new_state = gx[:, -(CONV_W - 1):]
    return (jax.nn.gelu(c) * u) @ w_d, new_state


def setup_inputs(seed: int = 0) -> dict:
    key = jax.random.key(seed)
    ks = iter(jax.random.split(key, 40))
    f32 = jnp.float32
    n_pages = PAST_LEN // PAGE_SIZE
    n_pool = (DEC_BATCH * n_pages * 5) // 4
    nrm = lambda shape, scale: jax.random.normal(next(ks), shape, f32) * scale
    raw = lambda shape: jax.random.normal(next(ks), shape, f32)
    gain = lambda: 1.0 + 0.1 * jax.random.normal(next(ks), (DEPTH, D_MODEL), f32)
    x_prompt = raw((BATCH, SEQ, D_MODEL))
    x_sample = raw((DEC_BATCH, DEC_SEQ, D_MODEL))
    cache_k_a = raw((DEPTH, n_pool, PAGE_SIZE, H_A, HD_A))
    cache_v_a = raw((DEPTH, n_pool, PAGE_SIZE, H_A, HD_A))
    cache_k_idx = raw((DEPTH, n_pool, PAGE_SIZE, D_IDX))
    cache_k_b = raw((DEPTH, n_pool, PAGE_SIZE, H_B, 2, HD_B))
    cache_v_b = raw((DEPTH, n_pool, PAGE_SIZE, H_B, DV_B))
    state_conv = raw((DEPTH, DEC_BATCH, CONV_W - 1, D_FF))
    page_table = jax.random.permutation(next(ks), n_pool)[: DEC_BATCH * n_pages].reshape(DEC_BATCH, n_pages).astype(jnp.int32)
    g_pre_attn = gain()
    w_in = nrm((DEPTH, D_MODEL, N_IN), D_MODEL ** -0.5)
    lambda_q1 = nrm((DEPTH, HD_B), 0.1)
    lambda_k1 = nrm((DEPTH, HD_B), 0.1)
    lambda_q2 = nrm((DEPTH, HD_B), 0.1)
    lambda_k2 = nrm((DEPTH, HD_B), 0.1)
    g_subln = 1.0 + 0.1 * jax.random.normal(next(ks), (DEPTH, DV_B), f32)
    w_o_a = nrm((DEPTH, W_A, D_MODEL), W_A ** -0.5)
    w_o_b = nrm((DEPTH, W_B, D_MODEL), W_B ** -0.5)
    w_out = nrm((DEPTH, D_MODEL, D_MODEL), D_MODEL ** -0.5)
    g_post_attn = gain()
    g_pre_ffn = gain()
    w_ffn_g = nrm((DEPTH, D_MODEL, D_FF), D_MODEL ** -0.5)
    w_ffn_u = nrm((DEPTH, D_MODEL, D_FF), D_MODEL ** -0.5)
    conv_w = nrm((DEPTH, CONV_W, D_FF), CONV_W ** -0.5)
    conv_b = nrm((DEPTH, D_FF), 0.02)
    w_ffn_d = nrm((DEPTH, D_FF, D_MODEL), D_FF ** -0.5)
    g_post_ffn = gain()
    return {"x_prompt": x_prompt, "x_sample": x_sample, "cache_k_a": cache_k_a, "cache_v_a": cache_v_a,
            "cache_k_idx": cache_k_idx, "cache_k_b": cache_k_b, "cache_v_b": cache_v_b,
            "state_conv": state_conv, "page_table": page_table, "g_pre_attn": g_pre_attn, "w_in": w_in,
            "lambda_q1": lambda_q1, "lambda_k1": lambda_k1, "lambda_q2": lambda_q2, "lambda_k2": lambda_k2,
            "g_subln": g_subln, "w_o_a": w_o_a, "w_o_b": w_o_b, "w_out": w_out, "g_post_attn": g_post_attn,
            "g_pre_ffn": g_pre_ffn, "w_ffn_g": w_ffn_g, "w_ffn_u": w_ffn_u, "conv_w": conv_w,
            "conv_b": conv_b, "w_ffn_d": w_ffn_d, "g_post_ffn": g_post_ffn}


def reference(x_prompt, x_sample, cache_k_a, cache_v_a, cache_k_idx, cache_k_b, cache_v_b, state_conv,
              page_table, g_pre_attn, w_in, lambda_q1, lambda_k1, lambda_q2, lambda_k2, g_subln,
              w_o_a, w_o_b, w_out, g_post_attn, g_pre_ffn, w_ffn_g, w_ffn_u, conv_w, conv_b, w_ffn_d,
              g_post_ffn):
    pos_p = jnp.arange(x_prompt.shape[1], dtype=jnp.int32)
    pos_s = page_table.shape[1] * PAGE_SIZE + jnp.arange(x_sample.shape[1], dtype=jnp.int32)
    xp, xs = x_prompt, x_sample
    rows_p, rows_s = [], []
    for l in range(DEPTH):
        lam_init = 0.8 - 0.6 * math.exp(-0.3 * l)
        f32 = jnp.float32
        lam = (jnp.exp(jnp.sum(lambda_q1[l].astype(f32) * lambda_k1[l].astype(f32)))
               - jnp.exp(jnp.sum(lambda_q2[l].astype(f32) * lambda_k2[l].astype(f32))) + lam_init)

        qa, ka, va, qi, ki, wi, qb, kb, vb, ga, gb = project(rms_norm(xp, g_pre_attn[l]), w_in[l], pos_p)
        o_a, o_b = prompt_mix(qa, ka, va, qi, ki, wi, qb, kb, vb, lam, lam_init, g_subln[l])
        mix = (jax.nn.sigmoid(ga) * (o_a @ w_o_a[l]) + jax.nn.sigmoid(gb) * (o_b @ w_o_b[l])) @ w_out[l]
        xp = xp + rms_norm(mix, g_post_attn[l])
        zero_state = jnp.zeros((xp.shape[0], CONV_W - 1, D_FF), xp.dtype)
        f, cs = conv_ffn(rms_norm(xp, g_pre_ffn[l]), zero_state, w_ffn_g[l], w_ffn_u[l], conv_w[l], conv_b[l], w_ffn_d[l])
        xp = xp + rms_norm(f, g_post_ffn[l])
        rows_p.append((ka, va, ki, kb, vb, cs))

        qa, ka, va, qi, ki, wi, qb, kb, vb, ga, gb = project(rms_norm(xs, g_pre_attn[l]), w_in[l], pos_s)
        o_a, o_b = sample_mix(qa, ka, va, qi, ki, wi, qb, kb, vb, lam, lam_init, g_subln[l],
                              cache_k_a, cache_v_a, cache_k_idx, cache_k_b, cache_v_b, page_table, l)
        mix = (jax.nn.sigmoid(ga) * (o_a @ w_o_a[l]) + jax.nn.sigmoid(gb) * (o_b @ w_o_b[l])) @ w_out[l]
        xs = xs + rms_norm(mix, g_post_attn[l])
        f, cs = conv_ffn(rms_norm(xs, g_pre_ffn[l]), state_conv[l], w_ffn_g[l], w_ffn_u[l], conv_w[l], conv_b[l], w_ffn_d[l])
        xs = xs + rms_norm(f, g_post_ffn[l])
        rows_s.append((ka, va, ki, kb, vb, cs))

    k_a_p = jnp.stack([r[0] for r in rows_p])
    v_a_p = jnp.stack([r[1] for r in rows_p])
    k_idx_p = jnp.stack([r[2] for r in rows_p])
    k_b_p = jnp.stack([r[3] for r in rows_p])
    v_b_p = jnp.stack([r[4] for r in rows_p])
    conv_p = jnp.stack([r[5] for r in rows_p])
    k_a_s = jnp.stack([r[0] for r in rows_s])
    v_a_s = jnp.stack([r[1] for r in rows_s])
    k_idx_s = jnp.stack([r[2] for r in rows_s])
    k_b_s = jnp.stack([r[3] for r in rows_s])
    v_b_s = jnp.stack([r[4] for r in rows_s])
    conv_s = jnp.stack([r[5] for r in rows_s])
    return (xp, xs, k_a_p, v_a_p, k_idx_p, k_b_p, v_b_p, conv_p, k_a_s, v_a_s, k_idx_s, k_b_s, v_b_s, conv_s)
```

```python
import functools
import math

import jax
import jax.numpy as jnp
import numpy as np
from jax import lax
from jax.experimental import pallas as pl
from jax.experimental.pallas import tpu as pltpu

H_A, HD_A = 8, 64
H_IDX, D_IDX = 4, 64
TOPK_MAX = 256
H_B, HD_B = 4, 64
DV_B = 2 * HD_B
CONV_W = 3
ROPE_THETA = 10000.0
EPS = 1e-6
W_A = H_A * HD_A
W_B = H_B * DV_B
W_QB = H_B * 2 * HD_B

LANES = 128
SUBLANES = 8
VMEM_LIMIT = 56 * 1024 * 1024

NEG = -1e30
INT_MIN = -2 ** 31

BF16 = jnp.bfloat16
F32 = jnp.float32
I32 = jnp.int32

_NT = (((1,), (1,)), ((), ()))


def _cparams(sem):
    return pltpu.CompilerParams(dimension_semantics=sem, vmem_limit_bytes=VMEM_LIMIT)


def _const_spec(shape):
    nd = len(shape)
    return pl.BlockSpec(shape, lambda *_: (0,) * nd, pipeline_mode=pl.Buffered(1))


_C_QA = 0
_C_KA = _C_QA + W_A
_C_QI = _C_KA + W_A
_C_QB = _C_QI + H_IDX * D_IDX
_C_KB = _C_QB + W_QB
_C_KW = _C_KB + W_QB
_C_VA = _C_KW + LANES
_C_VB = _C_VA + W_A
_C_GA = _C_VB + W_B


def _pack_w_in(w_in, d_model):
    sizes = [W_A, W_A, W_A, H_IDX * D_IDX, D_IDX, H_IDX, W_QB, W_QB, W_B, d_model, d_model]
    cuts = np.cumsum(sizes)[:-1]
    qa, ka, va, qi, ki, wi, qb, kb, vb, ga, gb = jnp.split(w_in, cuts, axis=1)
    pad = jnp.zeros((w_in.shape[0], LANES - D_IDX - H_IDX), w_in.dtype)
    return jnp.concatenate([qa, ka, qi, qb, kb, ki, wi, pad, va, vb, ga, gb], axis=1).astype(BF16)


def _rope_tables(pos):
    d = HD_A
    inv = ROPE_THETA ** (-jnp.arange(0, d, 2, dtype=F32) / d)
    ang = pos.astype(F32)[:, None] * inv
    cos, sin = jnp.cos(ang), jnp.sin(ang)
    return (jnp.concatenate([cos, cos, cos, cos], axis=1),
            jnp.concatenate([-sin, sin, -sin, sin], axis=1))


def _proj_kernel(x_ref, g_ref, w_ref, cos_ref, sin_ref, *outs, head_major, d_model):
    x = x_ref[...]
    tr = x.shape[0]
    ms = jnp.mean(x * x, axis=-1, keepdims=True)
    h = (x * lax.rsqrt(ms + EPS) * g_ref[...]).astype(BF16)
    cos = cos_ref[...]
    sin = sin_ref[...]
    lane = lax.broadcasted_iota(I32, (tr, LANES), 1)
    first_half = (lane % HD_A) < (HD_A // 2)

    def proj(c0, width):
        return jnp.dot(h, w_ref[:, c0:c0 + width], preferred_element_type=F32)

    def rope_slab(z, c, s):
        sw = jnp.where(first_half, pltpu.roll(z, LANES - HD_A // 2, 1), pltpu.roll(z, HD_A // 2, 1))
        return z * c + sw * s

    def rope(z):
        return jnp.concatenate([rope_slab(z[:, LANES * s:LANES * (s + 1)], cos, sin)
                                for s in range(z.shape[1] // LANES)], axis=1)

    qa = rope(proj(_C_QA, W_A)) * (HD_A ** -0.5)
    ka = rope(proj(_C_KA, W_A))
    qi = rope(proj(_C_QI, H_IDX * D_IDX))
    qb = rope(proj(_C_QB, W_QB)) * (HD_B ** -0.5)
    kb = rope(proj(_C_KB, W_QB))
    is_ki = lane < D_IDX
    kw = rope_slab(proj(_C_KW, LANES), jnp.where(is_ki, cos, 1.0), jnp.where(is_ki, sin, 0.0))
    kw = jnp.where(is_ki, kw, kw * (H_IDX ** -0.5 * D_IDX ** -0.5))
    va = proj(_C_VA, W_A)
    vb = proj(_C_VB, W_B)
    sga = jax.nn.sigmoid(proj(_C_GA, d_model))
    sgb = jax.nn.sigmoid(proj(_C_GA + d_model, d_model))

    if head_major:
        (qa_h, ka_o, ka_h, qi_h, qb_h, kb_o, kb_h, kw_o, ki_h, va_o, va_h, vb_o, vb_h, sga_o, sgb_o) = outs
        for hh in range(H_A):
            sl = slice(HD_A * hh, HD_A * (hh + 1))
            qa_h[hh] = qa[:, sl].astype(BF16)
            ka_h[hh] = ka[:, sl].astype(BF16)
            va_h[hh] = va[:, sl].astype(BF16)
            qb_h[hh] = qb[:, sl].astype(BF16)
            kb_h[hh] = kb[:, sl].astype(BF16)
        for hh in range(H_IDX):
            qi_h[hh] = qi[:, D_IDX * hh:D_IDX * (hh + 1)].astype(BF16)
        for hh in range(H_B):
            vb_h[hh] = vb[:, DV_B * hh:DV_B * (hh + 1)].astype(BF16)
        ki_h[...] = kw[:, :D_IDX].astype(BF16)
    else:
        (qa_o, ka_o, qi_o, qb_o, kb_o, kw_o, va_o, vb_o, sga_o, sgb_o) = outs
        qa_o[...] = qa
        qi_o[...] = qi
        qb_o[...] = qb
    ka_o[...] = ka
    kb_o[...] = kb
    kw_o[...] = kw
    va_o[...] = va
    vb_o[...] = vb
    sga_o[...] = sga
    sgb_o[...] = sgb


def _project(x, g, w_packed, cos, sin, head_major):
    rows, d_model = x.shape
    tr = min(256, rows)
    assert rows % tr == 0
    row = lambda w: pl.BlockSpec((tr, w), lambda i: (i, 0))
    hm = lambda n, w: pl.BlockSpec((n, tr, w), lambda i: (0, i, 0))
    f = lambda w: jax.ShapeDtypeStruct((rows, w), F32)
    hs = lambda n, w: jax.ShapeDtypeStruct((n, rows, w), BF16)
    if head_major:
        out_shape = [hs(H_A, HD_A), f(W_A), hs(H_A, HD_A), hs(H_IDX, D_IDX), hs(2 * H_B, HD_B), f(W_QB),
                     hs(2 * H_B, HD_B), f(LANES), jax.ShapeDtypeStruct((rows, D_IDX), BF16), f(W_A),
                     hs(H_A, HD_A), f(W_B), hs(H_B, DV_B), f(d_model), f(d_model)]
        out_specs = [hm(H_A, HD_A), row(W_A), hm(H_A, HD_A), hm(H_IDX, D_IDX), hm(2 * H_B, HD_B), row(W_QB),
                     hm(2 * H_B, HD_B), row(LANES), row(D_IDX), row(W_A),
                     hm(H_A, HD_A), row(W_B), hm(H_B, DV_B), row(d_model), row(d_model)]
    else:
        widths = [W_A, W_A, H_IDX * D_IDX, W_QB, W_QB, LANES, W_A, W_B, d_model, d_model]
        out_shape = [f(w) for w in widths]
        out_specs = [row(w) for w in widths]
    return pl.pallas_call(
        functools.partial(_proj_kernel, head_major=head_major, d_model=d_model),
        grid=(rows // tr,),
        in_specs=[row(d_model), _const_spec((1, d_model)), _const_spec(w_packed.shape), row(LANES), row(LANES)],
        out_specs=out_specs, out_shape=out_shape,
        compiler_params=_cparams(("parallel",)),
        name="proj_hm" if head_major else "proj_rm",
    )(x, g, w_packed, cos, sin)


def _sort_key(score, admissible):
    b = pltpu.bitcast(score + 0.0, I32)
    key = b ^ (lax.shift_right_arithmetic(b, 31) & 0x7FFFFFFF)
    return jnp.where(admissible, key, INT_MIN)


def _count(keys_ref, n_ch, rows, cmp, thr):
    width = keys_ref.shape[-1]

    def body(c, acc):
        blk = keys_ref[c]
        for s in range(width // LANES):
            acc = acc + jnp.where(cmp(blk[:, LANES * s:LANES * (s + 1)], thr), 1, 0)
        return acc

    acc = lax.fori_loop(0, n_ch, body, jnp.zeros((rows, LANES), I32))
    return jnp.sum(acc.astype(F32), axis=1, keepdims=True)


def _kth_largest(keys_ref, n_ch, rows, topk):
    def bit_body(it, tau):
        cand = tau + lax.shift_left(jnp.int32(1), 31 - it)
        cnt = _count(keys_ref, n_ch, rows, lambda k, t: k >= t, cand)
        return jnp.where(cnt >= topk, cand, tau)

    return lax.fori_loop(0, 32, bit_body, jnp.full((rows, LANES), INT_MIN, I32))


def _emit_selection(keys_ref, n_ch, rows, topk, write):
    width = keys_ref.shape[-1]
    tau = _kth_largest(keys_ref, n_ch, rows, topk)
    n_gt = _count(keys_ref, n_ch, rows, lambda k, t: k > t, tau)
    need = jnp.where(tau[:, :1] == INT_MIN, 0.0, topk - n_gt)
    r = lax.broadcasted_iota(I32, (LANES, LANES), 0)
    c = lax.broadcasted_iota(I32, (LANES, LANES), 1)
    before = jnp.where(r < c, 1.0, 0.0).astype(BF16)

    def body(ch, carry):
        blk = keys_ref[ch]
        out = []
        for s in range(width // LANES):
            k = blk[:, LANES * s:LANES * (s + 1)]
            eq = jnp.where(k == tau, 1.0, 0.0)
            rank = jnp.dot(eq.astype(BF16), before, preferred_element_type=F32) + carry
            tie_bias = jnp.where(rank < need, 0.0, NEG)
            out.append(jnp.where(k > tau, 0.0, jnp.where(k == tau, tie_bias, NEG)))
            carry = carry + jnp.sum(eq, axis=1, keepdims=True)
        write(ch, jnp.concatenate(out, axis=1) if len(out) > 1 else out[0])
        return carry

    lax.fori_loop(0, n_ch, body, jnp.zeros((rows, 1), F32))


def _select_p_kernel(qi_ref, kw_ref, ki_ref, bias_ref, keys_ref, *, tq, ck, topk):
    i = pl.program_id(0)
    n_all = keys_ref.shape[0]
    n_ch = ((i + 1) * tq + ck - 1) // ck
    w = kw_ref[:, D_IDX:D_IDX + H_IDX]
    wb = [jnp.broadcast_to(w[:, hh:hh + 1], (tq, ck)) for hh in range(H_IDX)]
    q_pos = i * tq + lax.broadcasted_iota(I32, (tq, ck), 0)
    col = lax.broadcasted_iota(I32, (tq, ck), 1)

    def score_body(c, _):
        k = ki_ref[pl.ds(pl.multiple_of(c * ck, ck), ck), :]
        score = jnp.zeros((tq, ck), F32)
        for hh in range(H_IDX):
            s = lax.dot_general(qi_ref[hh], k, _NT, preferred_element_type=F32)
            score = score + jnp.maximum(s, 0.0) * wb[hh]
        keys_ref[c] = _sort_key(score, col + c * ck <= q_pos)
        return 0

    lax.fori_loop(0, n_ch, score_body, 0)

    def write(ch, bias):
        bias_ref[ch] = bias.astype(BF16)

    _emit_selection(keys_ref, n_ch, tq, topk, write)

    def fill(ch, _):
        bias_ref[ch] = jnp.full((tq, ck), NEG, BF16)
        return 0

    lax.fori_loop(n_ch, n_all, fill, 0)


def _select_prompt(qi_h, kw, ki16, topk, tq, ck):
    seq = kw.shape[0]
    n_ch = seq // ck
    return pl.pallas_call(
        functools.partial(_select_p_kernel, tq=tq, ck=ck, topk=topk),
        grid=(seq // tq,),
        in_specs=[pl.BlockSpec((H_IDX, tq, D_IDX), lambda i: (0, i, 0)),
                  pl.BlockSpec((tq, LANES), lambda i: (i, 0)),
                  _const_spec(ki16.shape)],
        out_specs=pl.BlockSpec((n_ch, tq, ck), lambda i: (0, i, 0)),
        out_shape=jax.ShapeDtypeStruct((n_ch, seq, ck), BF16),
        scratch_shapes=[pltpu.VMEM((n_ch, tq, ck), I32)],
        compiler_params=_cparams(("parallel",)),
        name="select_prompt",
    )(qi_h, kw, ki16)


def _causal_steps(seq, tq, tk):
    qs, ks = [], []
    for i in range(seq // tq):
        for j in range(((i + 1) * tq - 1) // tk + 1):
            qs.append(i)
            ks.append(j)
    return np.asarray(qs, np.int32), np.asarray(ks, np.int32)


def _flash_update(s, v, m_ref, l_ref, acc_ref, idx):
    dv = v.shape[1]
    m_prev = m_ref[idx]
    m_new = jnp.maximum(m_prev, jnp.max(s, axis=1, keepdims=True))
    alpha = jnp.exp(m_prev - m_new)
    p = jnp.exp(s - m_new[:, :1])
    l_ref[idx] = alpha * l_ref[idx] + jnp.sum(p, axis=1, keepdims=True)
    scale = alpha[:, :dv] if dv <= LANES else jnp.concatenate([alpha] * (dv // LANES), axis=1)
    acc_ref[idx] = acc_ref[idx] * scale + jnp.dot(p.astype(BF16), v, preferred_element_type=F32)
    m_ref[idx] = m_new


def _flash_init(m_ref, l_ref, acc_ref):
    m_ref[...] = jnp.full(m_ref.shape, -jnp.inf, F32)
    l_ref[...] = jnp.zeros(l_ref.shape, F32)
    acc_ref[...] = jnp.zeros(acc_ref.shape, F32)


def _attn_a_kernel(qtab, ktab, q_ref, k_ref, v_ref, bias_ref, o_ref, m_ref, l_ref, acc_ref, *, tq, tk):
    step = pl.program_id(0)
    qi, kj = qtab[step], ktab[step]

    @pl.when(kj == 0)
    def _():
        _flash_init(m_ref, l_ref, acc_ref)

    bias = bias_ref[...].astype(F32)
    for hh in range(H_A):
        s = lax.dot_general(q_ref[hh], k_ref[hh], _NT, preferred_element_type=F32) + bias
        _flash_update(s, v_ref[hh], m_ref, l_ref, acc_ref, hh)

    @pl.when(kj == ((qi + 1) * tq - 1) // tk)
    def _():
        o_ref[...] = jnp.concatenate(
            [acc_ref[hh] / l_ref[hh][:, :HD_A] for hh in range(H_A)], axis=1).astype(o_ref.dtype)


def _lambda(lq1, lk1, lq2, lk2, lam_init):
    return (jnp.exp(jnp.sum(lq1[...] * lk1[...], axis=1, keepdims=True))
            - jnp.exp(jnp.sum(lq2[...] * lk2[...], axis=1, keepdims=True)) + lam_init)


def _subln(o, g, lam_init):
    y = o * lax.rsqrt(jnp.mean(o * o, axis=-1, keepdims=True) + EPS)
    return (y * g) * (1.0 - lam_init)


def _attn_b_kernel(qtab, ktab, q_ref, k_ref, v_ref, lq1, lk1, lq2, lk2, g_ref, o_ref, m_ref, l_ref, acc_ref,
                   *, tq, tk, lam_init):
    step = pl.program_id(0)
    qi, kj = qtab[step], ktab[step]

    @pl.when(kj == 0)
    def _():
        _flash_init(m_ref, l_ref, acc_ref)

    def run(masked):
        if masked:
            row = qi * tq + lax.broadcasted_iota(I32, (tq, tk), 0)
            col = kj * tk + lax.broadcasted_iota(I32, (tq, tk), 1)
            visible = col <= row
        for hm in range(2 * H_B):
            s = lax.dot_general(q_ref[hm], k_ref[hm], _NT, preferred_element_type=F32)
            if masked:
                s = jnp.where(visible, s, NEG)
            _flash_update(s, v_ref[hm // 2], m_ref, l_ref, acc_ref, hm)

    on_diagonal = (kj + 1) * tk - 1 > qi * tq
    pl.when(on_diagonal)(lambda: run(True))
    pl.when(jnp.logical_not(on_diagonal))(lambda: run(False))

    @pl.when(kj == ((qi + 1) * tq - 1) // tk)
    def _():
        lam = _lambda(lq1, lk1, lq2, lk2, lam_init)
        outs = []
        for hh in range(H_B):
            o1 = acc_ref[2 * hh] / l_ref[2 * hh]
            o2 = acc_ref[2 * hh + 1] / l_ref[2 * hh + 1]
            outs.append(_subln(o1 - lam * o2, g_ref[...], lam_init))
        o_ref[...] = jnp.concatenate(outs, axis=1).astype(o_ref.dtype)


def _attend_prompt_a(qa_h, ka_h, va_h, bias, tq, tk):
    seq = qa_h.shape[1]
    qs, ks = _causal_steps(seq, tq, tk)
    grid_spec = pltpu.PrefetchScalarGridSpec(
        num_scalar_prefetch=2, grid=(len(qs),),
        in_specs=[pl.BlockSpec((H_A, tq, HD_A), lambda s, qt, kt: (0, qt[s], 0)),
                  pl.BlockSpec((H_A, tk, HD_A), lambda s, qt, kt: (0, kt[s], 0)),
                  pl.BlockSpec((H_A, tk, HD_A), lambda s, qt, kt: (0, kt[s], 0)),
                  pl.BlockSpec((None, tq, tk), lambda s, qt, kt: (kt[s], qt[s], 0))],
        out_specs=pl.BlockSpec((tq, W_A), lambda s, qt, kt: (qt[s], 0)),
        scratch_shapes=[pltpu.VMEM((H_A, tq, LANES), F32), pltpu.VMEM((H_A, tq, LANES), F32),
                        pltpu.VMEM((H_A, tq, HD_A), F32)])
    return pl.pallas_call(
        functools.partial(_attn_a_kernel, tq=tq, tk=tk), grid_spec=grid_spec,
        out_shape=jax.ShapeDtypeStruct((seq, W_A), BF16),
        compiler_params=_cparams(("arbitrary",)), name="attn_a_prompt",
    )(jnp.asarray(qs), jnp.asarray(ks), qa_h, ka_h, va_h, bias)


def _attend_prompt_b(qb_h, kb_h, vb_h, lams, g_subln, lam_init, tq, tk):
    seq = qb_h.shape[1]
    qs, ks = _causal_steps(seq, tq, tk)
    vec = lambda w: pl.BlockSpec((1, w), lambda s, qt, kt: (0, 0))
    grid_spec = pltpu.PrefetchScalarGridSpec(
        num_scalar_prefetch=2, grid=(len(qs),),
        in_specs=[pl.BlockSpec((2 * H_B, tq, HD_B), lambda s, qt, kt: (0, qt[s], 0)),
                  pl.BlockSpec((2 * H_B, tk, HD_B), lambda s, qt, kt: (0, kt[s], 0)),
                  pl.BlockSpec((H_B, tk, DV_B), lambda s, qt, kt: (0, kt[s], 0)),
                  vec(HD_B), vec(HD_B), vec(HD_B), vec(HD_B), vec(DV_B)],
        out_specs=pl.BlockSpec((tq, W_B), lambda s, qt, kt: (qt[s], 0)),
        scratch_shapes=[pltpu.VMEM((2 * H_B, tq, LANES), F32), pltpu.VMEM((2 * H_B, tq, LANES), F32),
                        pltpu.VMEM((2 * H_B, tq, DV_B), F32)])
    return pl.pallas_call(
        functools.partial(_attn_b_kernel, tq=tq, tk=tk, lam_init=lam_init), grid_spec=grid_spec,
        out_shape=jax.ShapeDtypeStruct((seq, W_B), BF16),
        compiler_params=_cparams(("arbitrary",)), name="attn_b_prompt",
    )(jnp.asarray(qs), jnp.asarray(ks), qb_h, kb_h, vb_h, *lams, g_subln)


def _stack_heads(x, n, width):
    return jnp.concatenate([x[:, width * hh:width * (hh + 1)] for hh in range(n)], axis=0)


def _select_s_kernel(pt, qi_ref, kw_ref, *rest, npb, t_new, topk):
    pages = rest[:npb]
    bias_ref, keys_ref, q_sc, w_sc, knew_sc = rest[npb:]
    pg = pl.program_id(1)
    n_pg = pl.num_programs(1)
    page = pages[0].shape[0]
    rows = H_IDX * t_new
    n_slab = keys_ref.shape[0]

    @pl.when(pg == 0)
    def _():
        q_sc[...] = _stack_heads(qi_ref[...], H_IDX, D_IDX).astype(BF16)
        w = kw_ref[:, D_IDX:D_IDX + H_IDX]
        w_sc[...] = jnp.concatenate(
            [jnp.broadcast_to(w[:, hh:hh + 1], (t_new, LANES)) for hh in range(H_IDX)], axis=0)

    def scores(k16):
        s = lax.dot_general(q_sc[...], k16, _NT, preferred_element_type=F32)
        s = jnp.maximum(s, 0.0) * w_sc[...]
        tot = s[0:t_new]
        for hh in range(1, H_IDX):
            tot = tot + s[hh * t_new:(hh + 1) * t_new]
        return tot

    for r in range(npb):
        keys_ref[pg * npb + r] = _sort_key(scores(pages[r][...].astype(BF16)), True)

    @pl.when(pg == n_pg - 1)
    def _():
        knew_sc[...] = jnp.zeros(knew_sc.shape, F32)
        knew_sc[0:t_new, :] = kw_ref[:, :D_IDX]
        q_t = lax.broadcasted_iota(I32, (t_new, page), 0)
        k_t = lax.broadcasted_iota(I32, (t_new, page), 1)
        keys_ref[n_slab - 1] = _sort_key(scores(knew_sc[...].astype(BF16)), k_t <= q_t)

        def write(ch, bias):
            bias_ref[ch] = bias

        _emit_selection(keys_ref, n_slab, t_new, topk, write)


def _select_sample(page_table, qi, kw, cache_k_idx_l, topk, npb):
    db, n_pages = page_table.shape
    n_pool, page, _ = cache_k_idx_l.shape
    t_new = qi.shape[0] // db
    assert n_pages % npb == 0 and t_new == SUBLANES and page == LANES
    n_slab = n_pages + 1
    pspec = lambda r: pl.BlockSpec((None, page, D_IDX), lambda b, g, pt: (pt[b * n_pages + g * npb + r], 0, 0))
    grid_spec = pltpu.PrefetchScalarGridSpec(
        num_scalar_prefetch=1, grid=(db, n_pages // npb),
        in_specs=[pl.BlockSpec((t_new, H_IDX * D_IDX), lambda b, g, pt: (b, 0)),
                  pl.BlockSpec((t_new, LANES), lambda b, g, pt: (b, 0))] + [pspec(r) for r in range(npb)],
        out_specs=pl.BlockSpec((None, n_slab, t_new, page), lambda b, g, pt: (b, 0, 0, 0)),
        scratch_shapes=[pltpu.VMEM((n_slab, t_new, page), I32),
                        pltpu.VMEM((H_IDX * t_new, D_IDX), BF16),
                        pltpu.VMEM((H_IDX * t_new, LANES), F32),
                        pltpu.VMEM((page, D_IDX), F32)])
    return pl.pallas_call(
        functools.partial(_select_s_kernel, npb=npb, t_new=t_new, topk=topk), grid_spec=grid_spec,
        out_shape=jax.ShapeDtypeStruct((db, n_slab, t_new, page), F32),
        compiler_params=_cparams(("parallel", "arbitrary")), name="select_sample",
    )(page_table.reshape(-1), qi, kw, *([cache_k_idx_l] * npb))


def _block_diag(q, n, width):
    t = q.shape[0]
    tiled = jnp.concatenate([q] * n, axis=0)
    r = lax.broadcasted_iota(I32, tiled.shape, 0) // t
    c = lax.broadcasted_iota(I32, tiled.shape, 1) // width
    return jnp.where(r == c, tiled, 0.0)


def _diag_rows(o, n, t, width, first, stride):
    c = lax.broadcasted_iota(I32, (t, o.shape[1]), 1) // width
    out = jnp.zeros((t, o.shape[1]), F32)
    for hh in range(n):
        out = out + jnp.where(c == hh, o[first + hh * stride:first + hh * stride + t], 0.0)
    return out


def _attn_s_kernel(pt, qa_ref, qb_ref, ka_new, va_new, kb_new, vb_new, bias_ref, bias_new_ref,
                   lq1, lk1, lq2, lk2, g_ref, *rest, npb, t_new, lam_init):
    ka_p, va_p, kb_p, vb_p = (rest[i * npb:(i + 1) * npb] for i in range(4))
    (oa_ref, ob_ref, qa_sc, qb_sc, m_ref, l_ref, acc_ref, new_sc) = rest[4 * npb:]
    pg = pl.program_id(1)
    n_pg = pl.num_programs(1)
    page = ka_p[0].shape[0]
    rows_a = H_A * t_new

    @pl.when(pg == 0)
    def _():
        qa_sc[...] = _block_diag(qa_ref[...], H_A, HD_A).astype(BF16)
        qb_sc[...] = _block_diag(qb_ref[...], 2 * H_B, HD_B).astype(BF16)
        _flash_init(m_ref, l_ref, acc_ref)

    def tile_rows(b):
        return jnp.concatenate([b] * H_A, axis=0)

    def attend(ka, va, kb, vb, bias_a, bias_b):
        sa = lax.dot_general(qa_sc[...], ka, _NT, preferred_element_type=F32) + bias_a
        _flash_update(sa, va, m_ref, l_ref, acc_ref, 0)
        sb = lax.dot_general(qb_sc[...], kb, _NT, preferred_element_type=F32)
        if bias_b is not None:
            sb = sb + bias_b
        _flash_update(sb, vb, m_ref, l_ref, acc_ref, 1)

    for r in range(npb):
        attend(ka_p[r][...].astype(BF16), va_p[r][...].astype(BF16),
               kb_p[r][...].astype(BF16), vb_p[r][...].astype(BF16), tile_rows(bias_ref[r]), None)

    @pl.when(pg == n_pg - 1)
    def _():
        new_sc[...] = jnp.zeros(new_sc.shape, F32)
        for n, ref in enumerate((ka_new, va_new, kb_new, vb_new)):
            new_sc[n, 0:t_new, :] = ref[...]
        q_t = lax.broadcasted_iota(I32, (rows_a, page), 0) % t_new
        k_t = lax.broadcasted_iota(I32, (rows_a, page), 1)
        causal = jnp.where(k_t <= q_t, 0.0, NEG)
        attend(*(new_sc[n].astype(BF16) for n in range(4)), tile_rows(bias_new_ref[0]), causal)

        oa = acc_ref[0] / l_ref[0][:, :1]
        oa_ref[...] = _diag_rows(oa, H_A, t_new, HD_A, 0, t_new)
        ob = acc_ref[1] / l_ref[1][:, :1]
        o1 = _diag_rows(ob, H_B, t_new, DV_B, 0, 2 * t_new)
        o2 = _diag_rows(ob, H_B, t_new, DV_B, t_new, 2 * t_new)
        o = o1 - _lambda(lq1, lk1, lq2, lk2, lam_init) * o2
        ob_ref[...] = jnp.concatenate(
            [_subln(o[:, DV_B * hh:DV_B * (hh + 1)], g_ref[...], lam_init) for hh in range(H_B)], axis=1)


def _attend_sample(page_table, qa, qb, ka, va, kb, vb, bias, caches, lams, g_subln, lam_init, npb):
    db, n_pages = page_table.shape
    t_new = qa.shape[0] // db
    page = caches[0].shape[1]
    assert n_pages % npb == 0 and W_A == W_B == W_QB
    width = W_A
    rows = H_A * t_new
    assert rows == 2 * H_B * t_new
    tok = pl.BlockSpec((t_new, width), lambda b, g, pt: (b, 0))
    vec = lambda w: pl.BlockSpec((1, w), lambda b, g, pt: (0, 0))
    pspec = lambda r: pl.BlockSpec((None, page, width), lambda b, g, pt: (pt[b * n_pages + g * npb + r], 0, 0))
    in_specs = ([tok] * 6
                + [pl.BlockSpec((None, npb, t_new, page), lambda b, g, pt: (b, g, 0, 0)),
                   pl.BlockSpec((None, 1, t_new, page), lambda b, g, pt: (b, n_pages, 0, 0))]
                + [vec(HD_B)] * 4 + [vec(DV_B)]
                + [pspec(r) for _ in range(4) for r in range(npb)])
    grid_spec = pltpu.PrefetchScalarGridSpec(
        num_scalar_prefetch=1, grid=(db, n_pages // npb), in_specs=in_specs,
        out_specs=[tok, tok],
        scratch_shapes=[pltpu.VMEM((rows, width), BF16), pltpu.VMEM((rows, width), BF16),
                        pltpu.VMEM((2, rows, LANES), F32), pltpu.VMEM((2, rows, LANES), F32),
                        pltpu.VMEM((2, rows, width), F32), pltpu.VMEM((4, page, width), F32)])
    paged = [c for c in caches for _ in range(npb)]
    return pl.pallas_call(
        functools.partial(_attn_s_kernel, npb=npb, t_new=t_new, lam_init=lam_init), grid_spec=grid_spec,
        out_shape=[jax.ShapeDtypeStruct((db * t_new, width), F32)] * 2,
        compiler_params=_cparams(("parallel", "arbitrary")), name="attn_sample",
    )(page_table.reshape(-1), qa, qb, ka, va, kb, vb, bias, bias, *lams, g_subln, *paged)


def _rms(x, g):
    return x * lax.rsqrt(jnp.mean(x * x, axis=-1, keepdims=True) + EPS) * g


def _post_kernel(x_ref, oa_ref, ob_ref, sga_ref, sgb_ref, woa, wob, wout, g_pa, g_pf, wg, wu, cw, cb, wd, g_po,
                 *rest, carry, period):
    if carry:
        y_ref, gtail_ref, gbuf = rest
    else:
        prev1, prev2, y_ref, g_ref, gbuf = rest
    tr = x_ref.shape[0]
    halo = SUBLANES

    a = jnp.dot(oa_ref[...].astype(BF16), woa[...], preferred_element_type=F32)
    b = jnp.dot(ob_ref[...].astype(BF16), wob[...], preferred_element_type=F32)
    mixed = (sga_ref[...] * a + sgb_ref[...] * b).astype(BF16)
    mix = jnp.dot(mixed, wout[...], preferred_element_type=F32)
    x1 = x_ref[...] + _rms(mix, g_pa[...])

    h = _rms(x1, g_pf[...]).astype(BF16)
    g = jnp.dot(h, wg[...], preferred_element_type=F32)
    u = jnp.dot(h, wu[...], preferred_element_type=F32)

    if carry:
        @pl.when(pl.program_id(0) == 0)
        def _():
            gbuf[0:halo, :] = jnp.zeros((halo, gbuf.shape[1]), F32)
    else:
        gbuf[0:halo, :] = jnp.zeros((halo, gbuf.shape[1]), F32)
    gbuf[halo:halo + tr, :] = g
    m1 = gbuf[halo - 1:halo - 1 + tr, :]
    m2 = gbuf[halo - 2:halo - 2 + tr, :]
    if carry:
        tail = gbuf[tr:tr + halo, :]
        gbuf[0:halo, :] = tail
        gtail_ref[...] = tail
    else:
        t = lax.broadcasted_iota(I32, (tr, 1), 0) % period
        m1 = jnp.where(t == 0, prev1[...], m1)
        m2 = jnp.where(t < 2, prev2[...], m2)
        g_ref[...] = g
    c = cb[...]
    c = c + cw[0:1, :] * m2
    c = c + cw[1:2, :] * m1
    c = c + cw[2:3, :] * g
    gelu = c * (0.5 * (1.0 + jnp.tanh(math.sqrt(2.0 / math.pi) * (c + 0.044715 * (c * c * c)))))
    f = jnp.dot((gelu * u).astype(BF16), wd[...], preferred_element_type=F32)
    y_ref[...] = x1 + _rms(f, g_po[...])


def _post(x, oa, ob, sga, sgb, weights, prev=None, period=None):
    rows, d_model = x.shape
    woa, wob, wout, g_pa, g_pf, wg, wu, cw, cb, wd, g_po = weights
    d_ff = wg.shape[1]
    tr = min(256, rows)
    assert rows % tr == 0
    carry = prev is None
    row = lambda w: pl.BlockSpec((tr, w), lambda i: (i, 0))
    in_specs = ([row(d_model), row(W_A), row(W_B), row(d_model), row(d_model)]
                + [_const_spec(w.shape) for w in weights])
    args = [x, oa, ob, sga, sgb, *weights]
    if carry:
        out_specs = [row(d_model), pl.BlockSpec((SUBLANES, d_ff), lambda i: (0, 0))]
        out_shape = [jax.ShapeDtypeStruct((rows, d_model), F32), jax.ShapeDtypeStruct((SUBLANES, d_ff), F32)]
    else:
        in_specs += [row(d_ff), row(d_ff)]
        args += list(prev)
        out_specs = [row(d_model), row(d_ff)]
        out_shape = [jax.ShapeDtypeStruct((rows, d_model), F32), jax.ShapeDtypeStruct((rows, d_ff), F32)]
    return pl.pallas_call(
        functools.partial(_post_kernel, carry=carry, period=period),
        grid=(rows // tr,), in_specs=in_specs, out_specs=out_specs, out_shape=out_shape,
        scratch_shapes=[pltpu.VMEM((tr + SUBLANES, d_ff), F32)],
        compiler_params=_cparams(("arbitrary",)), name="post_prompt" if carry else "post_sample",
    )(*args)


def _pick(n, pref):
    t = min(pref, n)
    assert n % t == 0
    return t


def kernel(x_prompt, x_sample, cache_k_a, cache_v_a, cache_k_idx, cache_k_b, cache_v_b, state_conv, page_table, g_pre_attn, w_in, lambda_q1, lambda_k1, lambda_q2, lambda_k2, g_subln, w_o_a, w_o_b, w_out, g_post_attn, g_pre_ffn, w_ffn_g, w_ffn_u, conv_w, conv_b, w_ffn_d, g_post_ffn):
    batch, seq, d_model = x_prompt.shape
    db, t_new, _ = x_sample.shape
    depth, n_pool, page = cache_k_a.shape[:3]
    n_pages = page_table.shape[1]
    past = n_pages * page
    d_ff = w_ffn_g.shape[2]
    assert batch == 1, "the prompt kernels attend within one sequence"
    assert conv_w.shape[1] == CONV_W and t_new >= CONV_W - 1

    pos_p = jnp.arange(seq, dtype=I32)
    pos_s = past + jnp.arange(t_new, dtype=I32)
    cos_p, sin_p = _rope_tables(pos_p)
    cos_s, sin_s = _rope_tables(jnp.tile(pos_s, db))
    topk_p = min(TOPK_MAX, seq // 4)
    topk_s = min(TOPK_MAX, (past + t_new) // 4)

    xp = x_prompt.reshape(seq, d_model)
    xs = x_sample.reshape(db * t_new, d_model)
    rows_p, rows_s = [], []
    for l in range(depth):
        lam_init = 0.8 - 0.6 * math.exp(-0.3 * l)
        vec = lambda a: a[l].reshape(1, -1)
        w_packed = _pack_w_in(w_in[l], d_model)
        lams = (vec(lambda_q1), vec(lambda_k1), vec(lambda_q2), vec(lambda_k2))
        post_w = (w_o_a[l].astype(BF16), w_o_b[l].astype(BF16), w_out[l].astype(BF16), vec(g_post_attn),
                  vec(g_pre_ffn), w_ffn_g[l].astype(BF16), w_ffn_u[l].astype(BF16), conv_w[l], vec(conv_b),
                  w_ffn_d[l].astype(BF16), vec(g_post_ffn))

        (qa_h, ka, ka_h, qi_h, qb_h, kb, kb_h, kw, ki16, va, va_h, vb, vb_h, sga, sgb) = _project(
            xp, vec(g_pre_attn), w_packed, cos_p, sin_p, head_major=True)
        tk = _pick(seq, 512)
        bias = _select_prompt(qi_h, kw, ki16, topk_p, _pick(seq, 128), tk)
        o_a = _attend_prompt_a(qa_h, ka_h, va_h, bias, _pick(seq, 256), tk)
        o_b = _attend_prompt_b(qb_h, kb_h, vb_h, lams, vec(g_subln), lam_init, _pick(seq, 256), tk)
        xp, g_tail = _post(xp, o_a, o_b, sga, sgb, post_w)
        rows_p.append((ka.reshape(batch, seq, H_A, HD_A), va.reshape(batch, seq, H_A, HD_A),
                       kw[:, :D_IDX].reshape(batch, seq, D_IDX), kb.reshape(batch, seq, H_B, 2, HD_B),
                       vb.reshape(batch, seq, H_B, DV_B), g_tail[SUBLANES - (CONV_W - 1):].reshape(batch, CONV_W - 1, d_ff)))

        (qa, ka, qi, qb, kb, kw, va, vb, sga, sgb) = _project(
            xs, vec(g_pre_attn), w_packed, cos_s, sin_s, head_major=False)
        bias = _select_sample(page_table, qi, kw, cache_k_idx[l], topk_s, _pick(n_pages, 8))
        caches = (cache_k_a[l].reshape(n_pool, page, W_A), cache_v_a[l].reshape(n_pool, page, W_A),
                  cache_k_b[l].reshape(n_pool, page, W_QB), cache_v_b[l].reshape(n_pool, page, W_B))
        o_a, o_b = _attend_sample(page_table, qa, qb, ka, va, kb, vb, bias, caches, lams, vec(g_subln),
                                  lam_init, _pick(n_pages, 4))
        st = state_conv[l]
        zeros = jnp.zeros((db, t_new - 1, d_ff), F32)
        prev1 = jnp.concatenate([st[:, 1:2], zeros], axis=1).reshape(db * t_new, d_ff)
        prev2 = jnp.concatenate([st, zeros[:, 1:]], axis=1).reshape(db * t_new, d_ff)
        xs, g_all = _post(xs, o_a, o_b, sga, sgb, post_w, prev=(prev1, prev2), period=t_new)
        rows_s.append((ka.reshape(db, t_new, H_A, HD_A), va.reshape(db, t_new, H_A, HD_A),
                       kw[:, :D_IDX].reshape(db, t_new, D_IDX), kb.reshape(db, t_new, H_B, 2, HD_B),
                       vb.reshape(db, t_new, H_B, DV_B),
                       g_all.reshape(db, t_new, d_ff)[:, t_new - (CONV_W - 1):]))

    stack = lambda rows, n: jnp.stack([r[n] for r in rows])
    return (xp.reshape(batch, seq, d_model), xs.reshape(db, t_new, d_model),
            *[stack(rows_p, n) for n in range(6)], *[stack(rows_s, n) for n in range(6)])
```

```python
import functools
import math

import jax
import jax.numpy as jnp
import numpy as np
from jax import lax
from jax.experimental import pallas as pl
from jax.experimental.pallas import tpu as pltpu

H_A, HD_A = 8, 64
H_IDX, D_IDX = 4, 64
TOPK_MAX = 256
H_B, HD_B = 4, 64
DV_B = 2 * HD_B
CONV_W = 3
ROPE_THETA = 10000.0
EPS = 1e-6
W_A = H_A * HD_A
W_B = H_B * DV_B
W_QB = H_B * 2 * HD_B

LANES = 128
SUBLANES = 8
VMEM_LIMIT = 56 * 1024 * 1024

NEG = -1e30
INT_MIN = -2 ** 31

BF16 = jnp.bfloat16
F32 = jnp.float32
I32 = jnp.int32

_NT = (((1,), (1,)), ((), ()))


def _cparams(sem):
    return pltpu.CompilerParams(dimension_semantics=sem, vmem_limit_bytes=VMEM_LIMIT)


def _const_spec(shape):
    nd = len(shape)
    return pl.BlockSpec(shape, lambda *_: (0,) * nd, pipeline_mode=pl.Buffered(1))


_C_QA = 0
_C_KA = _C_QA + W_A
_C_QI = _C_KA + W_A
_C_QB = _C_QI + H_IDX * D_IDX
_C_KB = _C_QB + W_QB
_C_KW = _C_KB + W_QB
_C_VA = _C_KW + LANES
_C_VB = _C_VA + W_A
_C_GA = _C_VB + W_B


def _pack_w_in(w_in, d_model):
    sizes = [W_A, W_A, W_A, H_IDX * D_IDX, D_IDX, H_IDX, W_QB, W_QB, W_B, d_model, d_model]
    cuts = np.cumsum(sizes)[:-1]
    qa, ka, va, qi, ki, wi, qb, kb, vb, ga, gb = jnp.split(w_in, cuts, axis=1)
    pad = jnp.zeros((w_in.shape[0], LANES - D_IDX - H_IDX), w_in.dtype)
    return jnp.concatenate([qa, ka, qi, qb, kb, ki, wi, pad, va, vb, ga, gb], axis=1).astype(BF16)


def _rope_tables(pos):
    d = HD_A
    inv = ROPE_THETA ** (-jnp.arange(0, d, 2, dtype=F32) / d)
    ang = pos.astype(F32)[:, None] * inv
    cos, sin = jnp.cos(ang), jnp.sin(ang)
    return (jnp.concatenate([cos, cos, cos, cos], axis=1),
            jnp.concatenate([-sin, sin, -sin, sin], axis=1))


def _proj_kernel(x_ref, g_ref, w_ref, cos_ref, sin_ref, *outs, head_major, d_model):
    x = x_ref[...]
    tr = x.shape[0]
    ms = jnp.mean(x * x, axis=-1, keepdims=True)
    h = (x * lax.rsqrt(ms + EPS) * g_ref[...]).astype(BF16)
    cos = cos_ref[...]
    sin = sin_ref[...]
    lane = lax.broadcasted_iota(I32, (tr, LANES), 1)
    first_half = (lane % HD_A) < (HD_A // 2)

    def proj(c0, width):
        return jnp.dot(h, w_ref[:, c0:c0 + width], preferred_element_type=F32)

    def rope_slab(z, c, s):
        sw = jnp.where(first_half, pltpu.roll(z, LANES - HD_A // 2, 1), pltpu.roll(z, HD_A // 2, 1))
        return z * c + sw * s

    def rope(z):
        return jnp.concatenate([rope_slab(z[:, LANES * s:LANES * (s + 1)], cos, sin)
                                for s in range(z.shape[1] // LANES)], axis=1)

    qa = rope(proj(_C_QA, W_A)) * (HD_A ** -0.5)
    ka = rope(proj(_C_KA, W_A))
    qi = rope(proj(_C_QI, H_IDX * D_IDX))
    qb = rope(proj(_C_QB, W_QB)) * (HD_B ** -0.5)
    kb = rope(proj(_C_KB, W_QB))
    is_ki = lane < D_IDX
    kw = rope_slab(proj(_C_KW, LANES), jnp.where(is_ki, cos, 1.0), jnp.where(is_ki, sin, 0.0))
    kw = jnp.where(is_ki, kw, kw * (H_IDX ** -0.5 * D_IDX ** -0.5))
    va = proj(_C_VA, W_A)
    vb = proj(_C_VB, W_B)
    sga = jax.nn.sigmoid(proj(_C_GA, d_model))
    sgb = jax.nn.sigmoid(proj(_C_GA + d_model, d_model))

    if head_major:
        (qa_h, ka_o, ka_h, qi_h, qb_h, kb_o, kb_h, kw_o, ki_h, va_o, va_h, vb_o, vb_h, sga_o, sgb_o) = outs
        for hh in range(H_A):
            sl = slice(HD_A * hh, HD_A * (hh + 1))
            qa_h[hh] = qa[:, sl].astype(BF16)
            ka_h[hh] = ka[:, sl].astype(BF16)
            va_h[hh] = va[:, sl].astype(BF16)
            qb_h[hh] = qb[:, sl].astype(BF16)
            kb_h[hh] = kb[:, sl].astype(BF16)
        for hh in range(H_IDX):
            qi_h[hh] = qi[:, D_IDX * hh:D_IDX * (hh + 1)].astype(BF16)
        for hh in range(H_B):
            vb_h[hh] = vb[:, DV_B * hh:DV_B * (hh + 1)].astype(BF16)
        ki_h[...] = kw[:, :D_IDX].astype(BF16)
    else:
        (qa_o, ka_o, qi_o, qb_o, kb_o, kw_o, va_o, vb_o, sga_o, sgb_o) = outs
        qa_o[...] = qa
        qi_o[...] = qi
        qb_o[...] = qb
    ka_o[...] = ka
    kb_o[...] = kb
    kw_o[...] = kw
    va_o[...] = va
    vb_o[...] = vb
    sga_o[...] = sga
    sgb_o[...] = sgb


def _project(x, g, w_packed, cos, sin, head_major):
    rows, d_model = x.shape
    tr = min(256, rows)
    assert rows % tr == 0
    row = lambda w: pl.BlockSpec((tr, w), lambda i: (i, 0))
    hm = lambda n, w: pl.BlockSpec((n, tr, w), lambda i: (0, i, 0))
    f = lambda w: jax.ShapeDtypeStruct((rows, w), F32)
    hs = lambda n, w: jax.ShapeDtypeStruct((n, rows, w), BF16)
    if head_major:
        out_shape = [hs(H_A, HD_A), f(W_A), hs(H_A, HD_A), hs(H_IDX, D_IDX), hs(2 * H_B, HD_B), f(W_QB),
                     hs(2 * H_B, HD_B), f(LANES), jax.ShapeDtypeStruct((rows, D_IDX), BF16), f(W_A),
                     hs(H_A, HD_A), f(W_B), hs(H_B, DV_B), f(d_model), f(d_model)]
        out_specs = [hm(H_A, HD_A), row(W_A), hm(H_A, HD_A), hm(H_IDX, D_IDX), hm(2 * H_B, HD_B), row(W_QB),
                     hm(2 * H_B, HD_B), row(LANES), row(D_IDX), row(W_A),
                     hm(H_A, HD_A), row(W_B), hm(H_B, DV_B), row(d_model), row(d_model)]
    else:
        widths = [W_A, W_A, H_IDX * D_IDX, W_QB, W_QB, LANES, W_A, W_B, d_model, d_model]
        out_shape = [f(w) for w in widths]
        out_specs = [row(w) for w in widths]
    return pl.pallas_call(
        functools.partial(_proj_kernel, head_major=head_major, d_model=d_model),
        grid=(rows // tr,),
        in_specs=[row(d_model), _const_spec((1, d_model)), _const_spec(w_packed.shape), row(LANES), row(LANES)],
        out_specs=out_specs, out_shape=out_shape,
        compiler_params=_cparams(("parallel",)),
        name="proj_hm" if head_major else "proj_rm",
    )(x, g, w_packed, cos, sin)


def _sort_key(score, admissible):
    b = pltpu.bitcast(score + 0.0, I32)
    key = b ^ (lax.shift_right_arithmetic(b, 31) & 0x7FFFFFFF)
    return jnp.where(admissible, key, INT_MIN)


def _count(keys_ref, n_ch, rows, cmp, thr):
    width = keys_ref.shape[-1]

    def body(c, acc):
        blk = keys_ref[c]
        for s in range(width // LANES):
            acc = acc + jnp.where(cmp(blk[:, LANES * s:LANES * (s + 1)], thr), 1, 0)
        return acc

    acc = lax.fori_loop(0, n_ch, body, jnp.zeros((rows, LANES), I32))
    return jnp.sum(acc.astype(F32), axis=1, keepdims=True)


def _kth_largest(keys_ref, n_ch, rows, topk):
    def bit_body(it, tau):
        cand = tau + lax.shift_left(jnp.int32(1), 31 - it)
        cnt = _count(keys_ref, n_ch, rows, lambda k, t: k >= t, cand)
        return jnp.where(cnt >= topk, cand, tau)

    return lax.fori_loop(0, 32, bit_body, jnp.full((rows, LANES), INT_MIN, I32))


def _key_to_float(key):
    b = key ^ (lax.shift_right_arithmetic(key, 31) & 0x7FFFFFFF)
    return jnp.where(key == INT_MIN, -jnp.inf, pltpu.bitcast(b, F32))


MAX_SEARCH_STEPS = 128


def _threshold(keys_ref, gmax_ref, n_ch, rows, topk):
    width = keys_ref.shape[-1]

    def max_body(c, g):
        return jnp.maximum(g, keys_ref[c])

    gmax = lax.fori_loop(0, n_ch, max_body, jnp.full((rows, width), INT_MIN, I32))
    gmax_ref[0] = gmax
    lo = _kth_largest(gmax_ref, 1, rows, topk)
    row_max = jnp.max(_key_to_float(gmax), axis=1, keepdims=True)
    hi = jnp.broadcast_to(_sort_key(row_max, True) + 1, (rows, LANES))
    rep = lambda c: jnp.broadcast_to(c, (rows, LANES))
    c_lo = rep(_count(keys_ref, n_ch, rows, lambda k, t: k >= t, lo))
    c_hi = jnp.zeros((rows, LANES), F32)
    kf = float(topk)

    def open_rows(lo, hi, c_lo):
        return jnp.logical_and(c_lo != kf, hi - lo != 1)

    def cond(st):
        it, lo, hi, c_lo, c_hi = st
        n_open = jnp.max(jnp.where(open_rows(lo, hi, c_lo), 1.0, 0.0))
        return jnp.logical_and(n_open > 0.0, it < MAX_SEARCH_STEPS)

    def body(st):
        it, lo, hi, c_lo, c_hi = st
        d = hi - lo
        mid = (lax.shift_right_arithmetic(lo, 1) + lax.shift_right_arithmetic(hi, 1) + (lo & hi & 1))
        frac = (c_lo - kf + 0.5) / jnp.maximum(c_lo - c_hi, 1.0)
        step = jnp.minimum(frac * d.astype(F32), 2147483520.0).astype(I32)
        step = jnp.minimum(jnp.maximum(step, 1), d - 1)
        bisect = jnp.logical_or(d < 2, it % 3 == 2)
        cand = jnp.where(bisect, mid, lo + step)
        c = rep(_count(keys_ref, n_ch, rows, lambda k, t: k >= t, cand))
        live = open_rows(lo, hi, c_lo)
        up = jnp.logical_and(live, c >= kf)
        down = jnp.logical_and(live, c < kf)
        return (it + 1, jnp.where(up, cand, lo), jnp.where(down, cand, hi),
                jnp.where(up, c, c_lo), jnp.where(down, c, c_hi))

    _, lo, _, _, _ = lax.while_loop(cond, body, (jnp.int32(0), lo, hi, c_lo, c_hi))
    return lo


def _emit_selection(keys_ref, gmax_ref, n_ch, rows, topk, write):
    width = keys_ref.shape[-1]
    tau = _threshold(keys_ref, gmax_ref, n_ch, rows, topk)
    n_gt = _count(keys_ref, n_ch, rows, lambda k, t: k > t, tau)
    need = jnp.where(tau[:, :1] == INT_MIN, 0.0, topk - n_gt)
    r = lax.broadcasted_iota(I32, (LANES, LANES), 0)
    c = lax.broadcasted_iota(I32, (LANES, LANES), 1)
    before = jnp.where(r < c, 1.0, 0.0).astype(BF16)

    def body(ch, carry):
        blk = keys_ref[ch]
        out = []
        for s in range(width // LANES):
            k = blk[:, LANES * s:LANES * (s + 1)]
            eq = jnp.where(k == tau, 1.0, 0.0)
            rank = jnp.dot(eq.astype(BF16), before, preferred_element_type=F32) + carry
            tie_bias = jnp.where(rank < need, 0.0, NEG)
            out.append(jnp.where(k > tau, 0.0, jnp.where(k == tau, tie_bias, NEG)))
            carry = carry + jnp.sum(eq, axis=1, keepdims=True)
        write(ch, jnp.concatenate(out, axis=1) if len(out) > 1 else out[0])
        return carry

    lax.fori_loop(0, n_ch, body, jnp.zeros((rows, 1), F32))


def _select_p_kernel(qi_ref, kw_ref, ki_ref, bias_ref, keys_ref, gmax_ref, *, tq, ck, topk):
    i = pl.program_id(0)
    n_all = keys_ref.shape[0]
    n_ch = ((i + 1) * tq + ck - 1) // ck
    w = kw_ref[:, D_IDX:D_IDX + H_IDX]
    wb = [jnp.broadcast_to(w[:, hh:hh + 1], (tq, ck)) for hh in range(H_IDX)]
    q_pos = i * tq + lax.broadcasted_iota(I32, (tq, ck), 0)
    col = lax.broadcasted_iota(I32, (tq, ck), 1)

    def score_body(c, _):
        k = ki_ref[pl.ds(pl.multiple_of(c * ck, ck), ck), :]
        score = jnp.zeros((tq, ck), F32)
        for hh in range(H_IDX):
            s = lax.dot_general(qi_ref[hh], k, _NT, preferred_element_type=F32)
            score = score + jnp.maximum(s, 0.0) * wb[hh]
        keys_ref[c] = _sort_key(score, col + c * ck <= q_pos)
        return 0

    lax.fori_loop(0, n_ch, score_body, 0)

    def write(ch, bias):
        bias_ref[ch] = bias.astype(BF16)

    _emit_selection(keys_ref, gmax_ref, n_ch, tq, topk, write)

    def fill(ch, _):
        bias_ref[ch] = jnp.full((tq, ck), NEG, BF16)
        return 0

    lax.fori_loop(n_ch, n_all, fill, 0)


def _select_prompt(qi_h, kw, ki16, topk, tq, ck):
    seq = kw.shape[0]
    n_ch = seq // ck
    return pl.pallas_call(
        functools.partial(_select_p_kernel, tq=tq, ck=ck, topk=topk),
        grid=(seq // tq,),
        in_specs=[pl.BlockSpec((H_IDX, tq, D_IDX), lambda i: (0, i, 0)),
                  pl.BlockSpec((tq, LANES), lambda i: (i, 0)),
                  _const_spec(ki16.shape)],
        out_specs=pl.BlockSpec((n_ch, tq, ck), lambda i: (0, i, 0)),
        out_shape=jax.ShapeDtypeStruct((n_ch, seq, ck), BF16),
        scratch_shapes=[pltpu.VMEM((n_ch, tq, ck), I32), pltpu.VMEM((1, tq, ck), I32)],
        compiler_params=_cparams(("parallel",)),
        name="select_prompt",
    )(qi_h, kw, ki16)


def _causal_steps(seq, tq, tk):
    qs, ks = [], []
    for i in range(seq // tq):
        for j in range(((i + 1) * tq - 1) // tk + 1):
            qs.append(i)
            ks.append(j)
    return np.asarray(qs, np.int32), np.asarray(ks, np.int32)


def _softmax_step(s, m_ref, l_ref, idx, rows=slice(None)):
    m_prev = m_ref[idx, rows]
    m_new = jnp.maximum(m_prev, jnp.max(s, axis=1, keepdims=True))
    alpha = jnp.exp(m_prev - m_new)
    p = jnp.exp(s - m_new[:, :1])
    l_ref[idx, rows] = alpha * l_ref[idx, rows] + jnp.sum(p, axis=1, keepdims=True)
    m_ref[idx, rows] = m_new
    return p.astype(BF16), alpha


def _lanes(alpha, width):
    return alpha[:, :width] if width <= LANES else jnp.concatenate([alpha] * (width // LANES), axis=1)


def _flash_update(s, v, m_ref, l_ref, acc_ref, idx):
    p, alpha = _softmax_step(s, m_ref, l_ref, idx)
    acc_ref[idx] = acc_ref[idx] * _lanes(alpha, v.shape[1]) + jnp.dot(p, v, preferred_element_type=F32)


def _flash_init(m_ref, l_ref, acc_ref):
    m_ref[...] = jnp.full(m_ref.shape, -jnp.inf, F32)
    l_ref[...] = jnp.zeros(l_ref.shape, F32)
    acc_ref[...] = jnp.zeros(acc_ref.shape, F32)


def _attn_a_kernel(qtab, ktab, q_ref, k_ref, v_ref, bias_ref, o_ref, m_ref, l_ref, acc_ref, *, tq, tk):
    step = pl.program_id(0)
    qi, kj = qtab[step], ktab[step]

    @pl.when(kj == 0)
    def _():
        _flash_init(m_ref, l_ref, acc_ref)

    bias = bias_ref[...].astype(F32)
    for hh in range(H_A):
        s = lax.dot_general(q_ref[hh], k_ref[hh], _NT, preferred_element_type=F32) + bias
        _flash_update(s, v_ref[hh], m_ref, l_ref, acc_ref, hh)

    @pl.when(kj == ((qi + 1) * tq - 1) // tk)
    def _():
        o_ref[...] = jnp.concatenate(
            [acc_ref[hh] / l_ref[hh][:, :HD_A] for hh in range(H_A)], axis=1).astype(o_ref.dtype)


def _lambda(lq1, lk1, lq2, lk2, lam_init):
    return (jnp.exp(jnp.sum(lq1[...] * lk1[...], axis=1, keepdims=True))
            - jnp.exp(jnp.sum(lq2[...] * lk2[...], axis=1, keepdims=True)) + lam_init)


def _subln(o, g, lam_init):
    y = o * lax.rsqrt(jnp.mean(o * o, axis=-1, keepdims=True) + EPS)
    return (y * g) * (1.0 - lam_init)


def _attn_b_kernel(qtab, ktab, q_ref, k_ref, v_ref, lq1, lk1, lq2, lk2, g_ref, o_ref, m_ref, l_ref, acc_ref,
                   *, tq, tk, lam_init):
    step = pl.program_id(0)
    qi, kj = qtab[step], ktab[step]

    @pl.when(kj == 0)
    def _():
        _flash_init(m_ref, l_ref, acc_ref)

    def run(masked):
        if masked:
            row = qi * tq + lax.broadcasted_iota(I32, (tq, tk), 0)
            col = kj * tk + lax.broadcasted_iota(I32, (tq, tk), 1)
            visible = col <= row
        for hm in range(2 * H_B):
            s = lax.dot_general(q_ref[hm], k_ref[hm], _NT, preferred_element_type=F32)
            if masked:
                s = jnp.where(visible, s, NEG)
            _flash_update(s, v_ref[hm // 2], m_ref, l_ref, acc_ref, hm)

    on_diagonal = (kj + 1) * tk - 1 > qi * tq
    pl.when(on_diagonal)(lambda: run(True))
    pl.when(jnp.logical_not(on_diagonal))(lambda: run(False))

    @pl.when(kj == ((qi + 1) * tq - 1) // tk)
    def _():
        lam = _lambda(lq1, lk1, lq2, lk2, lam_init)
        outs = []
        for hh in range(H_B):
            o1 = acc_ref[2 * hh] / l_ref[2 * hh]
            o2 = acc_ref[2 * hh + 1] / l_ref[2 * hh + 1]
            outs.append(_subln(o1 - lam * o2, g_ref[...], lam_init))
        o_ref[...] = jnp.concatenate(outs, axis=1).astype(o_ref.dtype)


def _attend_prompt_a(qa_h, ka_h, va_h, bias, tq, tk):
    seq = qa_h.shape[1]
    qs, ks = _causal_steps(seq, tq, tk)
    grid_spec = pltpu.PrefetchScalarGridSpec(
        num_scalar_prefetch=2, grid=(len(qs),),
        in_specs=[pl.BlockSpec((H_A, tq, HD_A), lambda s, qt, kt: (0, qt[s], 0)),
                  pl.BlockSpec((H_A, tk, HD_A), lambda s, qt, kt: (0, kt[s], 0)),
                  pl.BlockSpec((H_A, tk, HD_A), lambda s, qt, kt: (0, kt[s], 0)),
                  pl.BlockSpec((None, tq, tk), lambda s, qt, kt: (kt[s], qt[s], 0))],
        out_specs=pl.BlockSpec((tq, W_A), lambda s, qt, kt: (qt[s], 0)),
        scratch_shapes=[pltpu.VMEM((H_A, tq, LANES), F32), pltpu.VMEM((H_A, tq, LANES), F32),
                        pltpu.VMEM((H_A, tq, HD_A), F32)])
    return pl.pallas_call(
        functools.partial(_attn_a_kernel, tq=tq, tk=tk), grid_spec=grid_spec,
        out_shape=jax.ShapeDtypeStruct((seq, W_A), BF16),
        compiler_params=_cparams(("arbitrary",)), name="attn_a_prompt",
    )(jnp.asarray(qs), jnp.asarray(ks), qa_h, ka_h, va_h, bias)


def _attend_prompt_b(qb_h, kb_h, vb_h, lams, g_subln, lam_init, tq, tk):
    seq = qb_h.shape[1]
    qs, ks = _causal_steps(seq, tq, tk)
    vec = lambda w: pl.BlockSpec((1, w), lambda s, qt, kt: (0, 0))
    grid_spec = pltpu.PrefetchScalarGridSpec(
        num_scalar_prefetch=2, grid=(len(qs),),
        in_specs=[pl.BlockSpec((2 * H_B, tq, HD_B), lambda s, qt, kt: (0, qt[s], 0)),
                  pl.BlockSpec((2 * H_B, tk, HD_B), lambda s, qt, kt: (0, kt[s], 0)),
                  pl.BlockSpec((H_B, tk, DV_B), lambda s, qt, kt: (0, kt[s], 0)),
                  vec(HD_B), vec(HD_B), vec(HD_B), vec(HD_B), vec(DV_B)],
        out_specs=pl.BlockSpec((tq, W_B), lambda s, qt, kt: (qt[s], 0)),
        scratch_shapes=[pltpu.VMEM((2 * H_B, tq, LANES), F32), pltpu.VMEM((2 * H_B, tq, LANES), F32),
                        pltpu.VMEM((2 * H_B, tq, DV_B), F32)])
    return pl.pallas_call(
        functools.partial(_attn_b_kernel, tq=tq, tk=tk, lam_init=lam_init), grid_spec=grid_spec,
        out_shape=jax.ShapeDtypeStruct((seq, W_B), BF16),
        compiler_params=_cparams(("arbitrary",)), name="attn_b_prompt",
    )(jnp.asarray(qs), jnp.asarray(ks), qb_h, kb_h, vb_h, *lams, g_subln)


def _stack_heads(x, n, width):
    return jnp.concatenate([x[:, width * hh:width * (hh + 1)] for hh in range(n)], axis=0)


def _select_s_kernel(pt, qi_ref, kw_ref, *rest, npb, t_new, topk):
    pages = rest[:npb]
    bias_ref, keys_ref, gmax_ref, q_sc, w_sc, knew_sc = rest[npb:]
    pg = pl.program_id(1)
    n_pg = pl.num_programs(1)
    page = pages[0].shape[1]
    n_slab = keys_ref.shape[0]

    @pl.when(pg == 0)
    def _():
        q_sc[...] = _stack_heads(qi_ref[...], H_IDX, D_IDX).astype(BF16)
        w = kw_ref[:, D_IDX:D_IDX + H_IDX]
        w_sc[...] = jnp.concatenate(
            [jnp.broadcast_to(w[:, hh:hh + 1], (t_new, LANES)) for hh in range(H_IDX)], axis=0)

    def head_sum(s):
        s = jnp.maximum(s, 0.0) * _lanes(w_sc[...], s.shape[1])
        tot = s[0:t_new]
        for hh in range(1, H_IDX):
            tot = tot + s[hh * t_new:(hh + 1) * t_new]
        return tot

    k_t = jnp.concatenate([p[...] for p in pages], axis=1).astype(BF16)
    tot = head_sum(jnp.dot(q_sc[...], k_t, preferred_element_type=F32))
    for r in range(npb):
        keys_ref[pg * npb + r] = _sort_key(tot[:, page * r:page * (r + 1)], True)

    @pl.when(pg == n_pg - 1)
    def _():
        knew_sc[...] = jnp.zeros(knew_sc.shape, F32)
        knew_sc[0:t_new, :] = kw_ref[:, :D_IDX]
        q_pos = lax.broadcasted_iota(I32, (t_new, page), 0)
        k_pos = lax.broadcasted_iota(I32, (t_new, page), 1)
        s_new = lax.dot_general(q_sc[...], knew_sc[...].astype(BF16), _NT, preferred_element_type=F32)
        keys_ref[n_slab - 1] = _sort_key(head_sum(s_new), k_pos <= q_pos)

        def write(ch, bias):
            bias_ref[ch] = bias

        _emit_selection(keys_ref, gmax_ref, n_slab, t_new, topk, write)


def _select_sample(page_table, qi, kw, k_idx_t, topk, npb):
    db, n_pages = page_table.shape
    n_pool, _, page = k_idx_t.shape
    t_new = qi.shape[0] // db
    assert n_pages % npb == 0 and t_new == SUBLANES and page == LANES
    n_slab = n_pages + 1
    pspec = lambda r: pl.BlockSpec((None, D_IDX, page), lambda b, g, pt: (pt[b * n_pages + g * npb + r], 0, 0))
    grid_spec = pltpu.PrefetchScalarGridSpec(
        num_scalar_prefetch=1, grid=(db, n_pages // npb),
        in_specs=[pl.BlockSpec((t_new, H_IDX * D_IDX), lambda b, g, pt: (b, 0)),
                  pl.BlockSpec((t_new, LANES), lambda b, g, pt: (b, 0))] + [pspec(r) for r in range(npb)],
        out_specs=pl.BlockSpec((None, n_slab, t_new, page), lambda b, g, pt: (b, 0, 0, 0)),
        scratch_shapes=[pltpu.VMEM((n_slab, t_new, page), I32),
                        pltpu.VMEM((1, t_new, page), I32),
                        pltpu.VMEM((H_IDX * t_new, D_IDX), BF16),
                        pltpu.VMEM((H_IDX * t_new, LANES), F32),
                        pltpu.VMEM((page, D_IDX), F32)])
    return pl.pallas_call(
        functools.partial(_select_s_kernel, npb=npb, t_new=t_new, topk=topk), grid_spec=grid_spec,
        out_shape=jax.ShapeDtypeStruct((db, n_slab, t_new, page), F32),
        compiler_params=_cparams(("parallel", "arbitrary")), name="select_sample",
    )(page_table.reshape(-1), qi, kw, *([k_idx_t] * npb))


def _block_diag(q, n, width):
    t = q.shape[0]
    tiled = jnp.concatenate([q] * n, axis=0)
    r = lax.broadcasted_iota(I32, tiled.shape, 0) // t
    c = lax.broadcasted_iota(I32, tiled.shape, 1) // width
    return jnp.where(r == c, tiled, 0.0)


def _diag_rows(o, n, t, width, first, stride):
    c = lax.broadcasted_iota(I32, (t, o.shape[1]), 1) // width
    out = jnp.zeros((t, o.shape[1]), F32)
    for hh in range(n):
        out = out + jnp.where(c == hh, o[first + hh * stride:first + hh * stride + t], 0.0)
    return out


def _attn_s_kernel(pt, qa_ref, qb_ref, ka_new, va_new, kb_new, vb_new, bias_ref, bias_new_ref,
                   lq1, lk1, lq2, lk2, g_ref, *rest, npb, t_new, lam_init):
    ka_p, va_p, kb_p, vb_p = (rest[i * npb:(i + 1) * npb] for i in range(4))
    (oa_ref, ob_ref, qa_sc, qb_sc, m_ref, l_ref, acca_ref, accb_ref, new_sc) = rest[4 * npb:]
    pg = pl.program_id(1)
    n_pg = pl.num_programs(1)
    page = ka_p[0].shape[1]
    rows = H_A * t_new
    pair = 2 * t_new

    @pl.when(pg == 0)
    def _():
        qa_sc[...] = _block_diag(qa_ref[...], H_A, HD_A).astype(BF16)
        qb_sc[...] = _block_diag(qb_ref[...], 2 * H_B, HD_B).astype(BF16)
        m_ref[...] = jnp.full(m_ref.shape, -jnp.inf, F32)
        l_ref[...] = jnp.zeros(l_ref.shape, F32)
        acca_ref[...] = jnp.zeros(acca_ref.shape, F32)
        accb_ref[...] = jnp.zeros(accb_ref.shape, F32)

    def tile_rows(b):
        return jnp.concatenate([b] * H_A, axis=0)

    def update_b(p, alpha, v_of_head):
        for hh in range(H_B):
            rs = slice(pair * hh, pair * (hh + 1))
            accb_ref[rs, :] = accb_ref[rs, :] * alpha[rs] + jnp.dot(p[rs], v_of_head(hh),
                                                                     preferred_element_type=F32)

    lane_cat = lambda refs: jnp.concatenate([r[...] for r in refs], axis=1).astype(BF16)
    bias = tile_rows(jnp.concatenate([bias_ref[r] for r in range(npb)], axis=1))
    sa = jnp.dot(qa_sc[...], lane_cat(ka_p), preferred_element_type=F32) + bias
    p, alpha = _softmax_step(sa, m_ref, l_ref, 0)
    acca_ref[...] = acca_ref[...] * _lanes(alpha, W_A) + lax.dot_general(
        p, lane_cat(va_p), _NT, preferred_element_type=F32)
    sb = jnp.dot(qb_sc[...], lane_cat(kb_p), preferred_element_type=F32)
    p, alpha = _softmax_step(sb, m_ref, l_ref, 1)
    update_b(p, alpha, lambda hh: jnp.concatenate([r[:, hh, :] for r in vb_p], axis=0).astype(BF16))

    @pl.when(pg == n_pg - 1)
    def _():
        new_sc[...] = jnp.zeros(new_sc.shape, F32)
        for n, ref in enumerate((ka_new, va_new, kb_new, vb_new)):
            new_sc[n, 0:t_new, :] = ref[...]
        k_new_a, v_new_a, k_new_b, v_new_b = (new_sc[n].astype(BF16) for n in range(4))
        sa = lax.dot_general(qa_sc[...], k_new_a, _NT, preferred_element_type=F32) + tile_rows(bias_new_ref[0])
        p, alpha = _softmax_step(sa, m_ref, l_ref, 0)
        acc_a = acca_ref[...] * _lanes(alpha, W_A) + jnp.dot(p, v_new_a, preferred_element_type=F32)
        q_pos = lax.broadcasted_iota(I32, (rows, page), 0) % t_new
        k_pos = lax.broadcasted_iota(I32, (rows, page), 1)
        sb = lax.dot_general(qb_sc[...], k_new_b, _NT, preferred_element_type=F32)
        p, alpha = _softmax_step(jnp.where(k_pos <= q_pos, sb, NEG), m_ref, l_ref, 1)
        update_b(p, alpha, lambda hh: v_new_b[:, DV_B * hh:DV_B * (hh + 1)])

        oa_ref[...] = _diag_rows(acc_a / l_ref[0][:, :1], H_A, t_new, HD_A, 0, t_new)
        ob = accb_ref[...] / l_ref[1]
        lam = _lambda(lq1, lk1, lq2, lk2, lam_init)
        ob_ref[...] = jnp.concatenate(
            [_subln(ob[pair * hh:pair * hh + t_new] - lam * ob[pair * hh + t_new:pair * (hh + 1)],
                    g_ref[...], lam_init) for hh in range(H_B)], axis=1)


def _attend_sample(page_table, qa, qb, ka, va, kb, vb, bias, caches, lams, g_subln, lam_init, npb):
    db, n_pages = page_table.shape
    t_new = qa.shape[0] // db
    page = caches[0].shape[2]
    assert n_pages % npb == 0 and W_A == W_B == W_QB
    width = W_A
    rows = H_A * t_new
    assert rows == 2 * H_B * t_new
    tok = pl.BlockSpec((t_new, width), lambda b, g, pt: (b, 0))
    vec = lambda w: pl.BlockSpec((1, w), lambda b, g, pt: (0, 0))
    phys = lambda b, g, pt, r: pt[b * n_pages + g * npb + r]
    kt_spec = lambda r: pl.BlockSpec((None, width, page), lambda b, g, pt: (phys(b, g, pt, r), 0, 0))
    vb_spec = lambda r: pl.BlockSpec((None, page, H_B, DV_B), lambda b, g, pt: (phys(b, g, pt, r), 0, 0, 0))
    in_specs = ([tok] * 6
                + [pl.BlockSpec((None, npb, t_new, page), lambda b, g, pt: (b, g, 0, 0)),
                   pl.BlockSpec((None, 1, t_new, page), lambda b, g, pt: (b, n_pages, 0, 0))]
                + [vec(HD_B)] * 4 + [vec(DV_B)]
                + [kt_spec(r) for _ in range(3) for r in range(npb)] + [vb_spec(r) for r in range(npb)])
    grid_spec = pltpu.PrefetchScalarGridSpec(
        num_scalar_prefetch=1, grid=(db, n_pages // npb), in_specs=in_specs,
        out_specs=[tok, tok],
        scratch_shapes=[pltpu.VMEM((rows, width), BF16), pltpu.VMEM((rows, width), BF16),
                        pltpu.VMEM((2, rows, LANES), F32), pltpu.VMEM((2, rows, LANES), F32),
                        pltpu.VMEM((rows, width), F32), pltpu.VMEM((rows, DV_B), F32),
                        pltpu.VMEM((4, page, width), F32)])
    paged = [c for c in caches for _ in range(npb)]
    return pl.pallas_call(
        functools.partial(_attn_s_kernel, npb=npb, t_new=t_new, lam_init=lam_init), grid_spec=grid_spec,
        out_shape=[jax.ShapeDtypeStruct((db * t_new, width), F32)] * 2,
        compiler_params=_cparams(("parallel", "arbitrary")), name="attn_sample",
    )(page_table.reshape(-1), qa, qb, ka, va, kb, vb, bias, bias, *lams, g_subln, *paged)


def _rms(x, g):
    return x * lax.rsqrt(jnp.mean(x * x, axis=-1, keepdims=True) + EPS) * g


def _post_kernel(x_ref, oa_ref, ob_ref, sga_ref, sgb_ref, woa, wob, wout, g_pa, g_pf, wg, wu, cw, cb, wd, g_po,
                 *rest, carry, period):
    if carry:
        y_ref, gtail_ref, gbuf = rest
    else:
        prev1, prev2, y_ref, g_ref, gbuf = rest
    tr = x_ref.shape[0]
    halo = SUBLANES

    a = jnp.dot(oa_ref[...].astype(BF16), woa[...], preferred_element_type=F32)
    b = jnp.dot(ob_ref[...].astype(BF16), wob[...], preferred_element_type=F32)
    mixed = (sga_ref[...] * a + sgb_ref[...] * b).astype(BF16)
    mix = jnp.dot(mixed, wout[...], preferred_element_type=F32)
    x1 = x_ref[...] + _rms(mix, g_pa[...])

    h = _rms(x1, g_pf[...]).astype(BF16)
    g = jnp.dot(h, wg[...], preferred_element_type=F32)
    u = jnp.dot(h, wu[...], preferred_element_type=F32)

    if carry:
        @pl.when(pl.program_id(0) == 0)
        def _():
            gbuf[0:halo, :] = jnp.zeros((halo, gbuf.shape[1]), F32)
    else:
        gbuf[0:halo, :] = jnp.zeros((halo, gbuf.shape[1]), F32)
    gbuf[halo:halo + tr, :] = g
    m1 = gbuf[halo - 1:halo - 1 + tr, :]
    m2 = gbuf[halo - 2:halo - 2 + tr, :]
    if carry:
        tail = gbuf[tr:tr + halo, :]
        gbuf[0:halo, :] = tail
        gtail_ref[...] = tail
    else:
        t = lax.broadcasted_iota(I32, (tr, 1), 0) % period
        m1 = jnp.where(t == 0, prev1[...], m1)
        m2 = jnp.where(t < 2, prev2[...], m2)
        g_ref[...] = g
    c = cb[...]
    c = c + cw[0:1, :] * m2
    c = c + cw[1:2, :] * m1
    c = c + cw[2:3, :] * g
    gelu = c * (0.5 * (1.0 + jnp.tanh(math.sqrt(2.0 / math.pi) * (c + 0.044715 * (c * c * c)))))
    f = jnp.dot((gelu * u).astype(BF16), wd[...], preferred_element_type=F32)
    y_ref[...] = x1 + _rms(f, g_po[...])


def _post(x, oa, ob, sga, sgb, weights, prev=None, period=None):
    rows, d_model = x.shape
    woa, wob, wout, g_pa, g_pf, wg, wu, cw, cb, wd, g_po = weights
    d_ff = wg.shape[1]
    tr = min(256, rows)
    assert rows % tr == 0
    carry = prev is None
    row = lambda w: pl.BlockSpec((tr, w), lambda i: (i, 0))
    in_specs = ([row(d_model), row(W_A), row(W_B), row(d_model), row(d_model)]
                + [_const_spec(w.shape) for w in weights])
    args = [x, oa, ob, sga, sgb, *weights]
    if carry:
        out_specs = [row(d_model), pl.BlockSpec((SUBLANES, d_ff), lambda i: (0, 0))]
        out_shape = [jax.ShapeDtypeStruct((rows, d_model), F32), jax.ShapeDtypeStruct((SUBLANES, d_ff), F32)]
    else:
        in_specs += [row(d_ff), row(d_ff)]
        args += list(prev)
        out_specs = [row(d_model), row(d_ff)]
        out_shape = [jax.ShapeDtypeStruct((rows, d_model), F32), jax.ShapeDtypeStruct((rows, d_ff), F32)]
    return pl.pallas_call(
        functools.partial(_post_kernel, carry=carry, period=period),
        grid=(rows // tr,), in_specs=in_specs, out_specs=out_specs, out_shape=out_shape,
        scratch_shapes=[pltpu.VMEM((tr + SUBLANES, d_ff), F32)],
        compiler_params=_cparams(("arbitrary",)), name="post_prompt" if carry else "post_sample",
    )(*args)


def _pick(n, pref):
    t = min(pref, n)
    assert n % t == 0
    return t


def kernel(x_prompt, x_sample, cache_k_a, cache_v_a, cache_k_idx, cache_k_b, cache_v_b, state_conv, page_table, g_pre_attn, w_in, lambda_q1, lambda_k1, lambda_q2, lambda_k2, g_subln, w_o_a, w_o_b, w_out, g_post_attn, g_pre_ffn, w_ffn_g, w_ffn_u, conv_w, conv_b, w_ffn_d, g_post_ffn):
    batch, seq, d_model = x_prompt.shape
    db, t_new, _ = x_sample.shape
    depth, n_pool, page = cache_k_a.shape[:3]
    n_pages = page_table.shape[1]
    past = n_pages * page
    d_ff = w_ffn_g.shape[2]
    assert batch == 1, "the prompt kernels attend within one sequence"
    assert conv_w.shape[1] == CONV_W and t_new >= CONV_W - 1

    pos_p = jnp.arange(seq, dtype=I32)
    pos_s = past + jnp.arange(t_new, dtype=I32)
    cos_p, sin_p = _rope_tables(pos_p)
    cos_s, sin_s = _rope_tables(jnp.tile(pos_s, db))
    topk_p = min(TOPK_MAX, seq // 4)
    topk_s = min(TOPK_MAX, (past + t_new) // 4)

    xp = x_prompt.reshape(seq, d_model)
    xs = x_sample.reshape(db * t_new, d_model)
    rows_p, rows_s = [], []
    for l in range(depth):
        lam_init = 0.8 - 0.6 * math.exp(-0.3 * l)
        vec = lambda a: a[l].reshape(1, -1)
        w_packed = _pack_w_in(w_in[l], d_model)
        lams = (vec(lambda_q1), vec(lambda_k1), vec(lambda_q2), vec(lambda_k2))
        post_w = (w_o_a[l].astype(BF16), w_o_b[l].astype(BF16), w_out[l].astype(BF16), vec(g_post_attn),
                  vec(g_pre_ffn), w_ffn_g[l].astype(BF16), w_ffn_u[l].astype(BF16), conv_w[l], vec(conv_b),
                  w_ffn_d[l].astype(BF16), vec(g_post_ffn))

        (qa_h, ka, ka_h, qi_h, qb_h, kb, kb_h, kw, ki16, va, va_h, vb, vb_h, sga, sgb) = _project(
            xp, vec(g_pre_attn), w_packed, cos_p, sin_p, head_major=True)
        tk = _pick(seq, 512)
        bias = _select_prompt(qi_h, kw, ki16, topk_p, _pick(seq, 128), tk)
        tq = _pick(seq, 512)
        o_a = _attend_prompt_a(qa_h, ka_h, va_h, bias, tq, tk)
        o_b = _attend_prompt_b(qb_h, kb_h, vb_h, lams, vec(g_subln), lam_init, tq, tk)
        xp, g_tail = _post(xp, o_a, o_b, sga, sgb, post_w)
        rows_p.append((ka.reshape(batch, seq, H_A, HD_A), va.reshape(batch, seq, H_A, HD_A),
                       kw[:, :D_IDX].reshape(batch, seq, D_IDX), kb.reshape(batch, seq, H_B, 2, HD_B),
                       vb.reshape(batch, seq, H_B, DV_B), g_tail[SUBLANES - (CONV_W - 1):].reshape(batch, CONV_W - 1, d_ff)))

        (qa, ka, qi, qb, kb, kw, va, vb, sga, sgb) = _project(
            xs, vec(g_pre_attn), w_packed, cos_s, sin_s, head_major=False)
        slots_minor = lambda c: jnp.moveaxis(c[l], 1, -1).reshape(n_pool, -1, page)
        bias = _select_sample(page_table, qi, kw, slots_minor(cache_k_idx), topk_s, _pick(n_pages, 8))
        caches = (slots_minor(cache_k_a), slots_minor(cache_v_a), slots_minor(cache_k_b), cache_v_b[l])
        o_a, o_b = _attend_sample(page_table, qa, qb, ka, va, kb, vb, bias, caches, lams, vec(g_subln),
                                  lam_init, _pick(n_pages, 8))
        st = state_conv[l]
        zeros = jnp.zeros((db, t_new - 1, d_ff), F32)
        prev1 = jnp.concatenate([st[:, 1:2], zeros], axis=1).reshape(db * t_new, d_ff)
        prev2 = jnp.concatenate([st, zeros[:, 1:]], axis=1).reshape(db * t_new, d_ff)
        xs, g_all = _post(xs, o_a, o_b, sga, sgb, post_w, prev=(prev1, prev2), period=t_new)
        rows_s.append((ka.reshape(db, t_new, H_A, HD_A), va.reshape(db, t_new, H_A, HD_A),
                       kw[:, :D_IDX].reshape(db, t_new, D_IDX), kb.reshape(db, t_new, H_B, 2, HD_B),
                       vb.reshape(db, t_new, H_B, DV_B),
                       g_all.reshape(db, t_new, d_ff)[:, t_new - (CONV_W - 1):]))

    stack = lambda rows, n: jnp.stack([r[n] for r in rows])
    return (xp.reshape(batch, seq, d_model), xs.reshape(db, t_new, d_model),
            *[stack(rows_p, n) for n in range(6)], *[stack(rows_s, n) for n in range(6)])
```
